```python
import math
import jax, jax.numpy as jnp
from jax import lax
import numpy as np

D_MODEL = 2048
BATCH = 16
SEQ = 256
DEPTH = 2
DEC_BATCH = 4
DEC_SEQ = 1024
PAST_LEN = 512

GRID_W = 64
N_EVEN = (DEPTH + 1) // 2
N_ODD = DEPTH // 2
EPS = 1e-6
ATT_HEADS = 16
ATT_KV_HEADS = 4
ATT_GROUPS = ATT_HEADS // ATT_KV_HEADS
HEAD_DIM = 64
WINDOW = 128
BLOCK = 128
ROPE_BASE = 10000.0
ATT_WIDTH = ATT_HEADS * HEAD_DIM
KV_WIDTH = ATT_KV_HEADS * HEAD_DIM
SGU_GROUPS = 8
SGU_GROUP_DIM = 128
SGU_CHUNK = 128
SGU_WIDTH = SGU_GROUPS * SGU_GROUP_DIM
EVEN_IN = ATT_WIDTH + 2 * KV_WIDTH + 2 * SGU_WIDTH
EVEN_OUT = ATT_WIDTH + SGU_WIDTH
EVEN_SPLITS = (ATT_WIDTH, ATT_WIDTH + KV_WIDTH, ATT_WIDTH + 2 * KV_WIDTH,
               ATT_WIDTH + 2 * KV_WIDTH + SGU_WIDTH)
RET_HEADS = 8
RET_DK = D_MODEL // RET_HEADS
RET_DV = 2 * RET_DK
RET_CHUNK = 128
RET_QK = RET_HEADS * RET_DK
RET_V = RET_HEADS * RET_DV
ODD_IN = 2 * RET_QK + 2 * RET_V
ODD_SPLITS = (RET_QK, 2 * RET_QK, 2 * RET_QK + RET_V)
FF_MULTIPLE = 256
D_FF = -(-8 * D_MODEL // (3 * FF_MULTIPLE)) * FF_MULTIPLE

kernel_name = 'hybrid_window_sgu_retention_diffusion_step'

F32 = jnp.float32


def rmsnorm(x, g):
    xf = x.astype(F32)
    y = xf * lax.rsqrt(jnp.mean(xf * xf, axis=-1, keepdims=True) + EPS)
    return (y * g.astype(F32)).astype(x.dtype)


def adaln(cvec, w, b):
    m = jax.nn.silu(cvec) @ w + b
    return jnp.split(m, 6, axis=-1)


def modulate(h, shift, scale):
    return h * (1 + scale[..., None, :]) + shift[..., None, :]


def axial_rope(L):
    rows = L // GRID_W
    t_row = jnp.repeat(jnp.arange(rows), GRID_W).astype(F32)
    t_col = jnp.tile(jnp.arange(GRID_W), rows).astype(F32)
    n_freq = HEAD_DIM // 4
    inv = ROPE_BASE ** (-jnp.arange(n_freq, dtype=F32) / n_freq)
    ang = jnp.stack([t_row[:, None] * inv, t_col[:, None] * inv], axis=1)
    return jnp.cos(ang), jnp.sin(ang)


def apply_rope(x, cos, sin):
    shp = x.shape
    xr = x.astype(F32).reshape(shp[:-1] + (2, 2, HEAD_DIM // 4))
    x1, x2 = xr[..., 0, :], xr[..., 1, :]
    bshape = (1, shp[1]) + (1,) * (len(shp) - 3) + (2, HEAD_DIM // 4)
    cb, sb = cos.reshape(bshape), sin.reshape(bshape)
    out = jnp.stack([x1 * cb - x2 * sb, x2 * cb + x1 * sb], axis=-2)
    return out.reshape(shp).astype(x.dtype)


def even_project(h, w_in):
    B, L, _ = h.shape
    z = h @ w_in
    q, k, v, u, vg = jnp.split(z, EVEN_SPLITS, axis=-1)
    q = q.reshape(B, L, ATT_KV_HEADS, ATT_GROUPS, HEAD_DIM)
    k = k.reshape(B, L, ATT_KV_HEADS, HEAD_DIM)
    v = v.reshape(B, L, ATT_KV_HEADS, HEAD_DIM)
    return q, k, v, jax.nn.gelu(u), jax.nn.gelu(vg)


def spatial_gating(u, vg, ln_g, ln_b, w_s, b_s):
    B, L, _ = u.shape
    nc = L // SGU_CHUNK
    vf = vg.astype(F32)
    mu = jnp.mean(vf, axis=-1, keepdims=True)
    var = jnp.mean(jnp.square(vf - mu), axis=-1, keepdims=True)
    vn = ((vf - mu) * lax.rsqrt(var + EPS) * ln_g.astype(F32) + ln_b.astype(F32)).astype(u.dtype)
    vc = vn.reshape(B, nc, SGU_CHUNK, SGU_GROUPS, SGU_GROUP_DIM)
    mixed = jnp.einsum('gpq,bnqgc->bnpgc', w_s, vc) + b_s.T[None, None, :, :, None]
    return u * mixed.reshape(B, L, SGU_WIDTH)


def attn_context(q, k, v, sink):
    B, L = q.shape[:2]
    s = jnp.einsum('blkgd,bmkd->bkglm', q, k).astype(F32) * (HEAD_DIM ** -0.5)
    sk = jnp.broadcast_to(sink.astype(F32)[None, :, :, None, None], s.shape[:-1] + (1,))
    p = jax.nn.softmax(jnp.concatenate([sk, s], axis=-1), axis=-1)[..., 1:]
    o = jnp.einsum('bkglm,bmkd->blkgd', p.astype(v.dtype), v)
    return o.reshape(B, L, ATT_WIDTH)


def attn_latent(q, k, v, k_ctx, v_ctx, sink):
    B, L = q.shape[:2]
    nb = L // BLOCK
    qb = q.reshape(B, nb, BLOCK, ATT_KV_HEADS, ATT_GROUPS, HEAD_DIM)
    pad = ((0, 0), (BLOCK, BLOCK), (0, 0), (0, 0))
    kp = jnp.pad(k, pad).reshape(B, nb + 2, BLOCK, ATT_KV_HEADS, HEAD_DIM)
    vp = jnp.pad(v, pad).reshape(B, nb + 2, BLOCK, ATT_KV_HEADS, HEAD_DIM)

    def band(xp):
        return jnp.concatenate([xp[:, :-2], xp[:, 1:-1], xp[:, 2:]], axis=2)

    kb, vb = band(kp), band(vp)
    scale = HEAD_DIM ** -0.5
    s_loc = jnp.einsum('bnqkgd,bnskd->bkgnqs', qb, kb).astype(F32) * scale
    qpos = jnp.arange(nb)[:, None, None] * BLOCK + jnp.arange(BLOCK)[None, :, None]
    kpos = (jnp.arange(nb)[:, None, None] - 1) * BLOCK + jnp.arange(3 * BLOCK)[None, None, :]
    valid = (jnp.abs(qpos - kpos) <= WINDOW) & (kpos >= 0) & (kpos < L)
    s_loc = jnp.where(valid, s_loc, -jnp.inf)
    s_ctx = jnp.einsum('bnqkgd,bmkd->bkgnqm', qb, k_ctx).astype(F32) * scale
    sk = jnp.broadcast_to(sink.astype(F32)[None, :, :, None, None, None], s_loc.shape[:-1] + (1,))
    p = jax.nn.softmax(jnp.concatenate([sk, s_loc, s_ctx], axis=-1), axis=-1)
    p_loc = p[..., 1:1 + 3 * BLOCK].astype(v.dtype)
    p_ctx = p[..., 1 + 3 * BLOCK:].astype(v.dtype)
    o = (jnp.einsum('bkgnqs,bnskd->bnqkgd', p_loc, vb)
         + jnp.einsum('bkgnqm,bmkd->bnqkgd', p_ctx, v_ctx))
    return o.reshape(B, L, ATT_WIDTH)


def even_mixer_context(h, w_in, w_out, sink, ln_g, ln_b, w_s, b_s):
    q, k, v, u, vg = even_project(h, w_in)
    sink_kg = sink.reshape(ATT_KV_HEADS, ATT_GROUPS)
    a = attn_context(q, k, v, sink_kg)
    g = spatial_gating(u, vg, ln_g, ln_b, w_s, b_s)
    return jnp.concatenate([a, g], axis=-1) @ w_out, k, v


def even_mixer_latent(h, k_ctx, v_ctx, cos, sin, w_in, w_out, sink, ln_g, ln_b, w_s, b_s):
    q, k, v, u, vg = even_project(h, w_in)
    q = apply_rope(q, cos, sin)
    k = apply_rope(k, cos, sin)
    sink_kg = sink.reshape(ATT_KV_HEADS, ATT_GROUPS)
    a = attn_latent(q, k, v, k_ctx, v_ctx, sink_kg)
    g = spatial_gating(u, vg, ln_g, ln_b, w_s, b_s)
    return jnp.concatenate([a, g], axis=-1) @ w_out


def retention_scan(q, k, v, log_g, s0):
    B, H, L, _ = q.shape
    C = RET_CHUNK
    nc = L // C
    idx = jnp.arange(C, dtype=F32)
    rel = idx[:, None] - idx[None, :]
    dmask = jnp.exp(jnp.where(rel >= 0, log_g[:, None, None] * rel, -jnp.inf))
    q_in = jnp.exp(log_g[:, None] * (idx + 1))[None, :, :, None]
    k_out = jnp.exp(log_g[:, None] * (C - 1 - idx))[None, :, :, None]
    g_chunk = jnp.exp(log_g * C)[None, :, None, None]

    def chunks(a):
        return jnp.moveaxis(a.reshape(B, H, nc, C, a.shape[-1]), 2, 0)

    def step(S, xs):
        qc, kc, vc = xs
        scores = jnp.einsum('bhid,bhjd->bhij', qc, kc) * dmask
        intra = jnp.einsum('bhij,bhje->bhie', scores, vc)
        inter = jnp.einsum('bhid,bhde->bhie', qc, S) * q_in
        S = S * g_chunk + jnp.einsum('bhjd,bhje->bhde', kc * k_out, vc)
        return S, intra + inter

    S, o = lax.scan(step, s0, (chunks(q), chunks(k), chunks(v)))
    o = jnp.moveaxis(o, 0, 2).reshape(B, H, L, v.shape[-1])
    return o, S


def retention_mixer(h, w_in, w_out, gn_g, dec_f, dec_b, s0_f, s0_b):
    B, L, _ = h.shape
    z = h @ w_in
    q, k, v, g = jnp.split(z, ODD_SPLITS, axis=-1)

    def heads(a, d):
        return a.astype(F32).reshape(B, L, RET_HEADS, d).transpose(0, 2, 1, 3)

    qh = heads(q, RET_DK)
    kh = heads(k, RET_DK) * (RET_DK ** -0.5)
    vh = heads(v, RET_DV)
    lg_f = jax.nn.log_sigmoid(dec_f.astype(F32))
    lg_b = jax.nn.log_sigmoid(dec_b.astype(F32))
    o_f, s_f = retention_scan(qh, kh, vh, lg_f, s0_f.astype(F32))
    flip = lambda a: jnp.flip(a, axis=2)
    o_b, s_b = retention_scan(flip(qh), flip(kh), flip(vh), lg_b, s0_b.astype(F32))
    o = o_f + flip(o_b)
    mu = jnp.mean(o, axis=-1, keepdims=True)
    var = jnp.mean(jnp.square(o - mu), axis=-1, keepdims=True)
    o = ((o - mu) * lax.rsqrt(var + EPS)).transpose(0, 2, 1, 3).reshape(B, L, RET_V)
    o = o * gn_g.astype(F32)
    out = (jax.nn.silu(g.astype(F32)) * o).astype(h.dtype) @ w_out
    return out, s_f.astype(h.dtype), s_b.astype(h.dtype)


def swiglu(h, w_gate, w_up, w_down):
    return (jax.nn.silu(h @ w_gate) * (h @ w_up)) @ w_down


def setup_inputs(seed: int = 0) -> dict:
    key = jax.random.key(seed)
    ks = iter(jax.random.split(key, 40))

    def nrm(shape, scale):
        return jax.random.normal(next(ks), shape, jnp.float32) * scale

    D = D_MODEL
    base = 1.0 - 2.0 ** (-5.0 - np.arange(RET_HEADS))
    decay_logit = jnp.asarray(np.log(base / (1.0 - base)).astype(np.float32))
    return {
        'x_prompt': nrm((BATCH, SEQ, D), 1.0),
        'x_sample': nrm((DEC_BATCH, DEC_SEQ, D), 1.0),
        'cache_attn_k': nrm((DEC_BATCH, N_EVEN, PAST_LEN, ATT_KV_HEADS, HEAD_DIM), 1.0),
        'cache_attn_v': nrm((DEC_BATCH, N_EVEN, PAST_LEN, ATT_KV_HEADS, HEAD_DIM), 1.0),
        'state_ret_fwd': nrm((DEC_BATCH, N_ODD, RET_HEADS, RET_DK, RET_DV), 0.5),
        'state_ret_bwd': nrm((DEC_BATCH, N_ODD, RET_HEADS, RET_DK, RET_DV), 0.5),
        'c': nrm((DEC_BATCH, D), 1.0),
        'c_ctx': nrm((D,), 1.0),
        'ada_w': nrm((DEPTH, D, 6 * D), 0.5 * D ** -0.5),
        'ada_b': nrm((DEPTH, 6 * D), 0.01),
        'norm_mix_g': 1.0 + nrm((DEPTH, D), 0.02),
        'norm_ffn_g': 1.0 + nrm((DEPTH, D), 0.02),
        'even_w_in': nrm((N_EVEN, D, EVEN_IN), D ** -0.5),
        'even_w_out': nrm((N_EVEN, EVEN_OUT, D), EVEN_OUT ** -0.5),
        'attn_sink': nrm((N_EVEN, ATT_HEADS), 0.5),
        'sgu_ln_g': 1.0 + nrm((N_EVEN, SGU_WIDTH), 0.02),
        'sgu_ln_b': nrm((N_EVEN, SGU_WIDTH), 0.01),
        'sgu_w': nrm((N_EVEN, SGU_GROUPS, SGU_CHUNK, SGU_CHUNK), SGU_CHUNK ** -0.5),
        'sgu_b': 1.0 + nrm((N_EVEN, SGU_GROUPS, SGU_CHUNK), 0.01),
        'ret_w_in': nrm((N_ODD, D, ODD_IN), D ** -0.5),
        'ret_w_out': nrm((N_ODD, RET_V, D), RET_V ** -0.5),
        'ret_gn_g': 1.0 + nrm((N_ODD, RET_V), 0.02),
        'ret_decay_fwd': decay_logit[None] + nrm((N_ODD, RET_HEADS), 0.1),
        'ret_decay_bwd': decay_logit[None] + nrm((N_ODD, RET_HEADS), 0.1),
        'ffn_w_gate': nrm((DEPTH, D, D_FF), D ** -0.5),
        'ffn_w_up': nrm((DEPTH, D, D_FF), D ** -0.5),
        'ffn_w_down': nrm((DEPTH, D_FF, D), D_FF ** -0.5),
        'final_g': 1.0 + nrm((D,), 0.02),
    }


def reference(x_prompt, x_sample, cache_attn_k, cache_attn_v, state_ret_fwd, state_ret_bwd,
              c, c_ctx, ada_w, ada_b, norm_mix_g, norm_ffn_g, even_w_in, even_w_out,
              attn_sink, sgu_ln_g, sgu_ln_b, sgu_w, sgu_b, ret_w_in, ret_w_out, ret_gn_g,
              ret_decay_fwd, ret_decay_bwd, ffn_w_gate, ffn_w_up, ffn_w_down, final_g):
    bp = x_prompt.shape[0]
    cos, sin = axial_rope(x_sample.shape[1])
    xp, xs = x_prompt, x_sample
    new_k, new_v, new_sf, new_sb = [], [], [], []
    for i in range(DEPTH):
        j = i // 2
        sh_p, sc_p, ga_p, sh2_p, sc2_p, ga2_p = adaln(c_ctx, ada_w[i], ada_b[i])
        sh_s, sc_s, ga_s, sh2_s, sc2_s, ga2_s = adaln(c, ada_w[i], ada_b[i])
        hp = modulate(rmsnorm(xp, norm_mix_g[i]), sh_p, sc_p)
        hs = modulate(rmsnorm(xs, norm_mix_g[i]), sh_s, sc_s)
        if i % 2 == 0:
            op, kc, vc = even_mixer_context(hp, even_w_in[j], even_w_out[j], attn_sink[j],
                                            sgu_ln_g[j], sgu_ln_b[j], sgu_w[j], sgu_b[j])
            os_ = even_mixer_latent(hs, cache_attn_k[:, j], cache_attn_v[:, j], cos, sin,
                                    even_w_in[j], even_w_out[j], attn_sink[j],
                                    sgu_ln_g[j], sgu_ln_b[j], sgu_w[j], sgu_b[j])
            new_k.append(kc)
            new_v.append(vc)
        else:
            zeros = jnp.zeros((bp, RET_HEADS, RET_DK, RET_DV), F32)
            op, sf, sb = retention_mixer(hp, ret_w_in[j], ret_w_out[j], ret_gn_g[j],
                                         ret_decay_fwd[j], ret_decay_bwd[j], zeros, zeros)
            os_, _, _ = retention_mixer(hs, ret_w_in[j], ret_w_out[j], ret_gn_g[j],
                                        ret_decay_fwd[j], ret_decay_bwd[j],
                                        state_ret_fwd[:, j], state_ret_bwd[:, j])
            new_sf.append(sf)
            new_sb.append(sb)
        xp = xp + ga_p[..., None, :] * op
        xs = xs + ga_s[..., None, :] * os_
        hp = modulate(rmsnorm(xp, norm_ffn_g[i]), sh2_p, sc2_p)
        hs = modulate(rmsnorm(xs, norm_ffn_g[i]), sh2_s, sc2_s)
        xp = xp + ga2_p[..., None, :] * swiglu(hp, ffn_w_gate[i], ffn_w_up[i], ffn_w_down[i])
        xs = xs + ga2_s[..., None, :] * swiglu(hs, ffn_w_gate[i], ffn_w_up[i], ffn_w_down[i])
    y_prompt = rmsnorm(xp, final_g)
    y_sample = rmsnorm(xs, final_g)
    new_attn_k = jnp.stack(new_k, axis=1)
    new_attn_v = jnp.stack(new_v, axis=1)
    new_ret_fwd = jnp.stack(new_sf, axis=1)
    new_ret_bwd = jnp.stack(new_sb, axis=1)
    return (y_prompt, y_sample, new_attn_k, new_attn_v, new_ret_fwd, new_ret_bwd)
```

```python
import functools
import math

import jax
import jax.numpy as jnp
from jax import lax
from jax.experimental import pallas as pl
from jax.experimental.pallas import tpu as pltpu

F32 = jnp.float32
BF16 = jnp.bfloat16

D_MODEL = 2048
BATCH = 16
SEQ = 256
DEPTH = 2
DEC_BATCH = 4
DEC_SEQ = 1024
PAST_LEN = 512
GRID_W = 64
EPS = 1e-6
ATT_HEADS = 16
ATT_KV_HEADS = 4
ATT_GROUPS = ATT_HEADS // ATT_KV_HEADS
HEAD_DIM = 64
WINDOW = 128
ROPE_BASE = 10000.0
ATT_WIDTH = ATT_HEADS * HEAD_DIM
KV_WIDTH = ATT_KV_HEADS * HEAD_DIM
SGU_GROUPS = 8
SGU_GROUP_DIM = 128
SGU_CHUNK = 128
SGU_WIDTH = SGU_GROUPS * SGU_GROUP_DIM
EVEN_IN = ATT_WIDTH + 2 * KV_WIDTH + 2 * SGU_WIDTH
RET_HEADS = 8
RET_DK = D_MODEL // RET_HEADS
RET_DV = 2 * RET_DK
RET_QK = RET_HEADS * RET_DK
RET_V = RET_HEADS * RET_DV
ODD_IN = 2 * RET_QK + 2 * RET_V
D_FF = 5632

M_PROMPT = BATCH * SEQ
M_SAMPLE = DEC_BATCH * DEC_SEQ
M_TOTAL = M_PROMPT + M_SAMPLE
N_COND = 1 + DEC_BATCH
COND_ROWS = 8

TM = 1024
EVEN_TN = 512
EVEN_NT = EVEN_IN // EVEN_TN
RET_CHUNK = 256
LOCAL_KEYS = 3 * WINDOW
NEG = -1e30
VMEM_LIMIT = 56 * 1024 * 1024


def _params(*sem):
    return pltpu.CompilerParams(dimension_semantics=sem, vmem_limit_bytes=VMEM_LIMIT)


def _silu(x):
    return x * jax.nn.sigmoid(x)


def _gelu_tanh(x):
    c = math.sqrt(2.0 / math.pi)
    return x * (0.5 * (1.0 + jnp.tanh(c * (x + 0.044715 * (x * x * x)))))


def _cond_group(m, tm):
    return jnp.maximum(1 + (m * tm - M_PROMPT) // DEC_SEQ, 0)


def _mod_spec(layer, which, tm, tn=D_MODEL):
    nblk = D_MODEL // tn
    return pl.BlockSpec((None, None, 1, tn),
                        lambda m, n: (layer, _cond_group(m, tm), 0, which * nblk + (n if nblk > 1 else 0)))


def _adaln_kernel(cv_ref, w_ref, b_ref, o_ref):
    s = _silu(cv_ref[...]).astype(BF16)
    o_ref[...] = jnp.dot(s, w_ref[...].astype(BF16), preferred_element_type=F32) + b_ref[...]


def _adaln(cv, ada_w, ada_b):
    tn = 1024
    n_out = 6 * D_MODEL
    return pl.pallas_call(
        _adaln_kernel,
        grid=(DEPTH, n_out // tn),
        in_specs=[pl.BlockSpec((COND_ROWS, D_MODEL), lambda l, n: (0, 0)),
                  pl.BlockSpec((None, D_MODEL, tn), lambda l, n: (l, 0, n)),
                  pl.BlockSpec((None, 1, tn), lambda l, n: (l, 0, n))],
        out_specs=pl.BlockSpec((None, COND_ROWS, tn), lambda l, n: (l, 0, n)),
        out_shape=jax.ShapeDtypeStruct((DEPTH, COND_ROWS, n_out), F32),
        compiler_params=_params("arbitrary", "arbitrary"),
        name="adaln",
    )(cv, ada_w, ada_b.reshape(DEPTH, 1, n_out))


def _norm_mod_to(h_ref, x_ref, g_ref, sh_ref, sc_ref, rows=256):
    g = g_ref[...]
    scale1 = 1.0 + sc_ref[...]
    shift = sh_ref[...]

    def body(i, carry):
        r = pl.multiple_of(i * rows, rows)
        x = x_ref[pl.ds(r, rows), :]
        ms = jnp.mean(x * x, axis=-1, keepdims=True)
        y = (x * lax.rsqrt(ms + EPS)) * g
        h_ref[pl.ds(r, rows), :] = (y * scale1 + shift).astype(BF16)
        return carry

    lax.fori_loop(0, x_ref.shape[0] // rows, body, 0)


def _inproj_kernel(x_ref, g_ref, sh_ref, sc_ref, w_ref, o_ref, h_ref, *, gelu_from):
    n = pl.program_id(1)

    @pl.when(n == 0)
    def _():
        _norm_mod_to(h_ref, x_ref, g_ref, sh_ref, sc_ref)

    acc = jnp.dot(h_ref[...], w_ref[...].astype(BF16), preferred_element_type=F32)
    if gelu_from is None:
        o_ref[...] = acc.astype(o_ref.dtype)
    else:
        @pl.when(n < gelu_from)
        def _():
            o_ref[...] = acc.astype(o_ref.dtype)

        @pl.when(n >= gelu_from)
        def _():
            o_ref[...] = _gelu_tanh(acc).astype(o_ref.dtype)


def _even_inproj(x, norm_g, mods, layer, w_in):
    tm, tn = TM, EVEN_TN
    return pl.pallas_call(
        functools.partial(_inproj_kernel, gelu_from=(ATT_WIDTH + 2 * KV_WIDTH) // tn),
        grid=(M_TOTAL // tm, EVEN_NT),
        in_specs=[pl.BlockSpec((tm, D_MODEL), lambda m, n: (m, 0)),
                  pl.BlockSpec((1, D_MODEL), lambda m, n: (0, 0)),
                  _mod_spec(layer, 0, tm), _mod_spec(layer, 1, tm),
                  pl.BlockSpec((D_MODEL, tn), lambda m, n: (0, n))],
        out_specs=pl.BlockSpec((None, tm, tn), lambda m, n: (n, m, 0)),
        out_shape=jax.ShapeDtypeStruct((EVEN_NT, M_TOTAL, tn), F32),
        scratch_shapes=[pltpu.VMEM((tm, D_MODEL), BF16)],
        compiler_params=_params("arbitrary", "arbitrary"),
        name="even_inproj",
    )(x, norm_g.reshape(1, D_MODEL), mods, mods, w_in)


def _ret_inproj(x, norm_g, mods, layer, w_in):
    tm, tn = TM, 512
    return pl.pallas_call(
        functools.partial(_inproj_kernel, gelu_from=None),
        grid=(M_TOTAL // tm, ODD_IN // tn),
        in_specs=[pl.BlockSpec((tm, D_MODEL), lambda m, n: (m, 0)),
                  pl.BlockSpec((1, D_MODEL), lambda m, n: (0, 0)),
                  _mod_spec(layer, 0, tm), _mod_spec(layer, 1, tm),
                  pl.BlockSpec((D_MODEL, tn), lambda m, n: (0, n))],
        out_specs=pl.BlockSpec((tm, tn), lambda m, n: (m, n)),
        out_shape=jax.ShapeDtypeStruct((M_TOTAL, ODD_IN), BF16),
        scratch_shapes=[pltpu.VMEM((tm, D_MODEL), BF16)],
        compiler_params=_params("arbitrary", "arbitrary"),
        name="ret_inproj",
    )(x, norm_g.reshape(1, D_MODEL), mods, mods, w_in)


def _outproj_kernel(*refs, n_lhs):
    lhs_refs = refs[:n_lhs]
    w_ref, x_ref, ga_ref, o_ref = refs[n_lhs:]
    acc = None
    off = 0
    for l_ref in lhs_refs:
        k = l_ref.shape[1]
        part = jnp.dot(l_ref[...], w_ref[off:off + k, :].astype(BF16), preferred_element_type=F32)
        acc = part if acc is None else acc + part
        off += k
    o_ref[...] = x_ref[...] + ga_ref[...] * acc


def _outproj(lhs_list, w, x, mods, layer, which, tn, name):
    tm = TM
    k_total = sum(l.shape[1] for l in lhs_list)
    assert w.shape == (k_total, D_MODEL)
    return pl.pallas_call(
        functools.partial(_outproj_kernel, n_lhs=len(lhs_list)),
        grid=(M_TOTAL // tm, D_MODEL // tn),
        in_specs=[pl.BlockSpec((tm, l.shape[1]), lambda m, n: (m, 0)) for l in lhs_list]
                 + [pl.BlockSpec((k_total, tn), lambda m, n: (0, n)),
                    pl.BlockSpec((tm, tn), lambda m, n: (m, n)),
                    _mod_spec(layer, which, tm, tn)],
        out_specs=pl.BlockSpec((tm, tn), lambda m, n: (m, n)),
        out_shape=jax.ShapeDtypeStruct((M_TOTAL, D_MODEL), F32),
        compiler_params=_params("arbitrary", "arbitrary"),
        name=name,
    )(*lhs_list, w, x, mods)


def _ffn_up_kernel(x_ref, g_ref, sh_ref, sc_ref, wg_ref, wu_ref, o_ref, h_ref):
    @pl.when(pl.program_id(1) == 0)
    def _():
        _norm_mod_to(h_ref, x_ref, g_ref, sh_ref, sc_ref)

    h = h_ref[...]
    a = jnp.dot(h, wg_ref[...].astype(BF16), preferred_element_type=F32)
    b = jnp.dot(h, wu_ref[...].astype(BF16), preferred_element_type=F32)
    o_ref[...] = (_silu(a) * b).astype(BF16)


def _ffn_up(x, norm_g, mods, layer, w_gate, w_up):
    tm, tn = TM, 512
    return pl.pallas_call(
        _ffn_up_kernel,
        grid=(M_TOTAL // tm, D_FF // tn),
        in_specs=[pl.BlockSpec((tm, D_MODEL), lambda m, n: (m, 0)),
                  pl.BlockSpec((1, D_MODEL), lambda m, n: (0, 0)),
                  _mod_spec(layer, 3, tm), _mod_spec(layer, 4, tm),
                  pl.BlockSpec((D_MODEL, tn), lambda m, n: (0, n)),
                  pl.BlockSpec((D_MODEL, tn), lambda m, n: (0, n))],
        out_specs=pl.BlockSpec((tm, tn), lambda m, n: (m, n)),
        out_shape=jax.ShapeDtypeStruct((M_TOTAL, D_FF), BF16),
        scratch_shapes=[pltpu.VMEM((tm, D_MODEL), BF16)],
        compiler_params=_params("arbitrary", "arbitrary"),
        name="ffn_up",
    )(x, norm_g.reshape(1, D_MODEL), mods, mods, w_gate, w_up)


def _softmax_pv(s_list, v_list, sink_col):
    m = sink_col
    for s in s_list:
        m = jnp.maximum(m, jnp.max(s, axis=-1, keepdims=True))
    den = jnp.exp(sink_col - m)
    o = None
    for s, v in zip(s_list, v_list):
        e = jnp.exp(s - m)
        den = den + jnp.sum(e, axis=-1, keepdims=True)
        pv = jnp.dot(e.astype(BF16), v, preferred_element_type=F32)
        o = pv if o is None else o + pv
    return o / den


def _sink_col(sink_ref, kh, rows):
    return jnp.concatenate(
        [jnp.full((rows, 1), sink_ref[kh * ATT_GROUPS + g], F32) for g in range(ATT_GROUPS)], axis=0)


_NT = (((1,), (1,)), ((), ()))


def _attn_ctx_kernel(sink_ref, q_ref, kv_ref, a_ref):
    scale = HEAD_DIM ** -0.5
    outs = []
    for kh in range(ATT_KV_HEADS):
        k = kv_ref[:, kh * HEAD_DIM:(kh + 1) * HEAD_DIM].astype(BF16)
        v = kv_ref[:, KV_WIDTH + kh * HEAD_DIM:KV_WIDTH + (kh + 1) * HEAD_DIM].astype(BF16)
        qs = []
        for g in range(ATT_GROUPS):
            c0 = (kh % 2) * 256 + g * HEAD_DIM
            qs.append(q_ref[kh // 2, :, c0:c0 + HEAD_DIM])
        q4 = (jnp.concatenate(qs, axis=0) * scale).astype(BF16)
        s = lax.dot_general(q4, k, _NT, preferred_element_type=F32)
        o = _softmax_pv([s], [v], _sink_col(sink_ref, kh, SEQ))
        outs += [o[g * SEQ:(g + 1) * SEQ] for g in range(ATT_GROUPS)]
    a_ref[...] = jnp.concatenate(outs, axis=1).astype(BF16)


def _attn_ctx(z, sink):
    return pl.pallas_call(
        _attn_ctx_kernel,
        grid=(BATCH,),
        in_specs=[pl.BlockSpec(memory_space=pltpu.SMEM),
                  pl.BlockSpec((2, SEQ, EVEN_TN), lambda b: (0, b, 0)),
                  pl.BlockSpec((None, SEQ, EVEN_TN), lambda b: (2, b, 0))],
        out_specs=pl.BlockSpec((SEQ, ATT_WIDTH), lambda b: (b, 0)),
        out_shape=jax.ShapeDtypeStruct((M_PROMPT, ATT_WIDTH), BF16),
        compiler_params=_params("arbitrary"),
        name="attn_ctx",
    )(sink, z, z)


def _rope_slab(x, cos, sin_signed, first_half):
    up = pltpu.roll(x, 128 - 16, axis=1)
    dn = pltpu.roll(x, 16, axis=1)
    return x * cos + jnp.where(first_half, up, dn) * sin_signed


def _attn_lat_kernel(sink_ref, q_ref, kv_ref, ck_ref, cv_ref, cos_ref, sin_ref, a_ref,
                     kr_scr, vb_scr, ckb_scr, cvb_scr):
    L = DEC_SEQ
    QB = 128
    lane = lax.broadcasted_iota(jnp.int32, (1, 128), 1)
    first_half = (lane % 32) < 16
    scale = HEAD_DIM ** -0.5

    for j in range(KV_WIDTH // 128):
        kr = _rope_slab(kv_ref[:, j * 128:(j + 1) * 128], cos_ref[...], sin_ref[...], first_half)
        for t in range(2):
            kr_scr[2 * j + t] = kr[:, t * HEAD_DIM:(t + 1) * HEAD_DIM].astype(BF16)
    for kh in range(ATT_KV_HEADS):
        lo, hi = kh * HEAD_DIM, (kh + 1) * HEAD_DIM
        vb_scr[kh] = kv_ref[:, KV_WIDTH + lo:KV_WIDTH + hi].astype(BF16)
        ckb_scr[kh] = ck_ref[:, lo:hi].astype(BF16)
        cvb_scr[kh] = cv_ref[:, lo:hi].astype(BF16)

    def qblock(qi, carry):
        r0 = pl.multiple_of(qi * QB, QB)
        start = pl.multiple_of(jnp.clip((qi - 1) * QB, 0, L - LOCAL_KEYS), QB)
        cos = cos_ref[pl.ds(r0, QB), :]
        sin = sin_ref[pl.ds(r0, QB), :]
        qr = []
        for s in range(ATT_WIDTH // 128):
            slab = q_ref[s // 4, pl.ds(r0, QB), (s % 4) * 128:(s % 4 + 1) * 128]
            qr.append(_rope_slab(slab, cos, sin, first_half) * scale)
        row = lax.broadcasted_iota(jnp.int32, (ATT_GROUPS * QB, LOCAL_KEYS), 0)
        col = lax.broadcasted_iota(jnp.int32, (ATT_GROUPS * QB, LOCAL_KEYS), 1)
        valid = jnp.abs((r0 + (row & (QB - 1))) - (start + col)) <= WINDOW
        outs = []
        for kh in range(ATT_KV_HEADS):
            qs = []
            for g in range(ATT_GROUPS):
                slab = qr[2 * kh + g // 2]
                qs.append(slab[:, (g % 2) * HEAD_DIM:(g % 2 + 1) * HEAD_DIM])
            q4 = jnp.concatenate(qs, axis=0).astype(BF16)
            kl = kr_scr[kh, pl.ds(start, LOCAL_KEYS), :]
            vl = vb_scr[kh, pl.ds(start, LOCAL_KEYS), :]
            s_loc = lax.dot_general(q4, kl, _NT, preferred_element_type=F32)
            s_loc = jnp.where(valid, s_loc, NEG)
            s_ctx = lax.dot_general(q4, ckb_scr[kh], _NT, preferred_element_type=F32)
            o = _softmax_pv([s_loc, s_ctx], [vl, cvb_scr[kh]], _sink_col(sink_ref, kh, QB))
            outs += [o[g * QB:(g + 1) * QB] for g in range(ATT_GROUPS)]
        a_ref[pl.ds(r0, QB), :] = jnp.concatenate(outs, axis=1).astype(BF16)
        return carry

    lax.fori_loop(0, L // QB, qblock, 0)


def _rope_tables():
    pos = jnp.arange(DEC_SEQ)
    t_row = (pos // GRID_W).astype(F32)
    t_col = (pos % GRID_W).astype(F32)
    n_freq = HEAD_DIM // 4
    inv = ROPE_BASE ** (-jnp.arange(n_freq, dtype=F32) / n_freq)
    a_row = t_row[:, None] * inv
    a_col = t_col[:, None] * inv
    cos64 = jnp.concatenate([jnp.cos(a_row), jnp.cos(a_row), jnp.cos(a_col), jnp.cos(a_col)], axis=1)
    sin64 = jnp.concatenate([-jnp.sin(a_row), jnp.sin(a_row), -jnp.sin(a_col), jnp.sin(a_col)], axis=1)
    return jnp.tile(cos64, (1, 2)), jnp.tile(sin64, (1, 2))


def _attn_lat(z, sink, cache_k, cache_v):
    L = DEC_SEQ
    rb0 = M_PROMPT // L
    cos, sin = _rope_tables()
    return pl.pallas_call(
        _attn_lat_kernel,
        grid=(DEC_BATCH,),
        in_specs=[pl.BlockSpec(memory_space=pltpu.SMEM),
                  pl.BlockSpec((2, L, EVEN_TN), lambda b: (0, rb0 + b, 0)),
                  pl.BlockSpec((None, L, EVEN_TN), lambda b: (2, rb0 + b, 0)),
                  pl.BlockSpec((None, PAST_LEN, KV_WIDTH), lambda b: (b, 0, 0)),
                  pl.BlockSpec((None, PAST_LEN, KV_WIDTH), lambda b: (b, 0, 0)),
                  pl.BlockSpec((L, 128), lambda b: (0, 0)),
                  pl.BlockSpec((L, 128), lambda b: (0, 0))],
        out_specs=pl.BlockSpec((L, ATT_WIDTH), lambda b: (b, 0)),
        out_shape=jax.ShapeDtypeStruct((M_SAMPLE, ATT_WIDTH), BF16),
        scratch_shapes=[pltpu.VMEM((ATT_KV_HEADS, L, HEAD_DIM), BF16),
                        pltpu.VMEM((ATT_KV_HEADS, L, HEAD_DIM), BF16),
                        pltpu.VMEM((ATT_KV_HEADS, PAST_LEN, HEAD_DIM), BF16),
                        pltpu.VMEM((ATT_KV_HEADS, PAST_LEN, HEAD_DIM), BF16)],
        compiler_params=_params("arbitrary"),
        name="attn_lat",
    )(sink, z, z, cache_k, cache_v, cos, sin)


def _sgu_kernel(u0_ref, u1_ref, v0_ref, v1_ref, lng_ref, lnb_ref, ws_ref, bst_ref, o_ref):
    half = SGU_WIDTH // 2
    per_half = SGU_GROUPS // 2

    def chunk(c, carry):
        r = pl.multiple_of(c * SGU_CHUNK, SGU_CHUNK)
        va = v0_ref[pl.ds(r, SGU_CHUNK), :]
        vb = v1_ref[pl.ds(r, SGU_CHUNK), :]
        mu = (jnp.sum(va, axis=-1, keepdims=True) + jnp.sum(vb, axis=-1, keepdims=True)) / SGU_WIDTH
        da = va - mu
        db = vb - mu
        var = (jnp.sum(da * da, axis=-1, keepdims=True) + jnp.sum(db * db, axis=-1, keepdims=True)) / SGU_WIDTH
        rs = lax.rsqrt(var + EPS)
        vn = [(da * rs * lng_ref[:, :half] + lnb_ref[:, :half]).astype(BF16),
              (db * rs * lng_ref[:, half:] + lnb_ref[:, half:]).astype(BF16)]
        u_refs = [u0_ref, u1_ref]
        for g in range(SGU_GROUPS):
            lo = (g % per_half) * SGU_GROUP_DIM
            mixed = jnp.dot(ws_ref[g].astype(BF16), vn[g // per_half][:, lo:lo + SGU_GROUP_DIM],
                            preferred_element_type=F32) + bst_ref[:, g:g + 1]
            u = u_refs[g // per_half][pl.ds(r, SGU_CHUNK), lo:lo + SGU_GROUP_DIM]
            o_ref[pl.ds(r, SGU_CHUNK), g * SGU_GROUP_DIM:(g + 1) * SGU_GROUP_DIM] = (u * mixed).astype(BF16)
        return carry

    lax.fori_loop(0, o_ref.shape[0] // SGU_CHUNK, chunk, 0)


def _sgu(z, ln_g, ln_b, w_s, b_s):
    tm = 512
    zspec = lambda t: pl.BlockSpec((None, tm, EVEN_TN), lambda m: (t, m, 0))
    return pl.pallas_call(
        _sgu_kernel,
        grid=(M_TOTAL // tm,),
        in_specs=[zspec(3), zspec(4), zspec(5), zspec(6),
                  pl.BlockSpec((1, SGU_WIDTH), lambda m: (0, 0)),
                  pl.BlockSpec((1, SGU_WIDTH), lambda m: (0, 0)),
                  pl.BlockSpec((SGU_GROUPS, SGU_CHUNK, SGU_CHUNK), lambda m: (0, 0, 0)),
                  pl.BlockSpec((SGU_CHUNK, SGU_GROUPS), lambda m: (0, 0))],
        out_specs=pl.BlockSpec((tm, SGU_WIDTH), lambda m: (m, 0)),
        out_shape=jax.ShapeDtypeStruct((M_TOTAL, SGU_WIDTH), BF16),
        compiler_params=_params("arbitrary"),
        name="sgu",
    )(z, z, z, z, ln_g.reshape(1, SGU_WIDTH), ln_b.reshape(1, SGU_WIDTH), w_s, b_s.T)


def _log_sigmoid(x):
    return jnp.minimum(x, 0.0) - jnp.log1p(jnp.exp(-jnp.abs(x)))


_TN = (((0,), (0,)), ((), ()))


def _ret_kernel(*refs, L, has_s0, emit_states):
    C = RET_CHUNK
    nc = L // C
    decf_ref, decb_ref, q_ref, k_ref, v_ref, g_ref, gn_ref = refs[:7]
    pos = 7
    if has_s0:
        s0f_ref, s0b_ref = refs[pos:pos + 2]
        pos += 2
    og_ref = refs[pos]
    pos += 1
    if emit_states:
        sf_ref, sb_ref = refs[pos:pos + 2]
        pos += 2
    o_scr = refs[pos]
    if not emit_states:
        sf_ref, sb_ref = refs[pos + 1:pos + 3]

    h = pl.program_id(1)
    lgf = _log_sigmoid(jnp.full((1, 1), decf_ref[h], F32))
    lgb = _log_sigmoid(jnp.full((1, 1), decb_ref[h], F32))
    kscale = RET_DK ** -0.5

    ri = lax.broadcasted_iota(jnp.int32, (C, C), 0)
    ci = lax.broadcasted_iota(jnp.int32, (C, C), 1)
    rel = (ri - ci).astype(F32)
    dmask = (jnp.where(rel >= 0, jnp.exp(lgf * jnp.maximum(rel, 0.0)), 0.0)
             + jnp.where(rel <= 0, jnp.exp(lgb * jnp.maximum(-rel, 0.0)), 0.0)) * kscale
    idx = lax.broadcasted_iota(jnp.int32, (C, 1), 0).astype(F32)
    qin_f = jnp.exp(lgf * (idx + 1.0))
    kout_f = jnp.exp(lgf * (C - 1.0 - idx)) * kscale
    gch_f = jnp.exp(lgf * float(C))
    qin_b = jnp.exp(lgb * (C - idx))
    kout_b = jnp.exp(lgb * idx) * kscale
    gch_b = jnp.exp(lgb * float(C))

    def state_update(s_ref, have_state, k_rows, kout, gch, v_rows):
        kd = (k_rows.astype(F32) * kout).astype(BF16)
        upd = lax.dot_general(kd, v_rows, _TN, preferred_element_type=F32)
        s_ref[...] = s_ref[...] * gch + upd if have_state else upd

    if has_s0:
        sb_ref[...] = s0b_ref[...]
    for c in reversed(range(nc)):
        rows = slice(c * C, (c + 1) * C)
        have_b = has_s0 or c < nc - 1
        if have_b:
            o_scr[rows, :] = jnp.dot(q_ref[rows, :], sb_ref[...].astype(BF16),
                                     preferred_element_type=F32) * qin_b
        if emit_states or c > 0:
            state_update(sb_ref, have_b, k_ref[rows, :], kout_b, gch_b, v_ref[rows, :])

    if has_s0:
        sf_ref[...] = s0f_ref[...]
    for c in range(nc):
        rows = slice(c * C, (c + 1) * C)
        q = q_ref[rows, :]
        k = k_ref[rows, :]
        v = v_ref[rows, :]
        s = lax.dot_general(q, k, _NT, preferred_element_type=F32) * dmask
        o = jnp.dot(s.astype(BF16), v, preferred_element_type=F32)
        have_f = has_s0 or c > 0
        if have_f:
            o = o + jnp.dot(q, sf_ref[...].astype(BF16), preferred_element_type=F32) * qin_f
        if has_s0 or c < nc - 1:
            o = o + o_scr[rows, :]
        mu = jnp.mean(o, axis=-1, keepdims=True)
        d = o - mu
        var = jnp.mean(d * d, axis=-1, keepdims=True)
        on = (d * lax.rsqrt(var + EPS)) * gn_ref[...]
        gate = g_ref[rows, :].astype(F32)
        og_ref[rows, :] = (_silu(gate) * on).astype(BF16)
        if emit_states or c < nc - 1:
            state_update(sf_ref, have_f, k, kout_f, gch_f, v)


def _retention(z, gn_g, dec_f, dec_b, L, row_block0, n_req, s0=None, emit_states=False):
    has_s0 = s0 is not None
    kb0 = RET_QK // RET_DK
    vb0 = 2 * RET_QK // RET_DV
    gb0 = (2 * RET_QK + RET_V) // RET_DV
    state_spec = pl.BlockSpec((None, None, None, RET_DK, RET_DV), lambda b, h: (b, 0, h, 0, 0))
    in_specs = [pl.BlockSpec(memory_space=pltpu.SMEM), pl.BlockSpec(memory_space=pltpu.SMEM),
                pl.BlockSpec((L, RET_DK), lambda b, h: (row_block0 + b, h)),
                pl.BlockSpec((L, RET_DK), lambda b, h: (row_block0 + b, kb0 + h)),
                pl.BlockSpec((L, RET_DV), lambda b, h: (row_block0 + b, vb0 + h)),
                pl.BlockSpec((L, RET_DV), lambda b, h: (row_block0 + b, gb0 + h)),
                pl.BlockSpec((1, RET_DV), lambda b, h: (0, h))]
    args = [dec_f, dec_b, z, z, z, z, gn_g.reshape(1, RET_V)]
    if has_s0:
        in_specs += [state_spec, state_spec]
        args += list(s0)
    out_specs = [pl.BlockSpec((L, RET_DV), lambda b, h: (b, h))]
    out_shape = [jax.ShapeDtypeStruct((n_req * L, RET_V), BF16)]
    scratch = [pltpu.VMEM((L, RET_DV), F32)]
    if emit_states:
        out_specs += [state_spec, state_spec]
        out_shape += [jax.ShapeDtypeStruct((n_req, 1, RET_HEADS, RET_DK, RET_DV), F32)] * 2
    else:
        scratch += [pltpu.VMEM((RET_DK, RET_DV), F32)] * 2
    return pl.pallas_call(
        functools.partial(_ret_kernel, L=L, has_s0=has_s0, emit_states=emit_states),
        grid=(n_req, RET_HEADS),
        in_specs=in_specs,
        out_specs=out_specs,
        out_shape=out_shape,
        scratch_shapes=scratch,
        compiler_params=_params("arbitrary", "arbitrary"),
        name="retention_prompt" if emit_states else "retention_latent",
    )(*args)


def _final_norm_kernel(x_ref, g_ref, o_ref):
    x = x_ref[...]
    ms = jnp.mean(x * x, axis=-1, keepdims=True)
    o_ref[...] = (x * lax.rsqrt(ms + EPS)) * g_ref[...]


def _final_norm(x, g):
    tm = 512
    return pl.pallas_call(
        _final_norm_kernel,
        grid=(M_TOTAL // tm,),
        in_specs=[pl.BlockSpec((tm, D_MODEL), lambda m: (m, 0)),
                  pl.BlockSpec((1, D_MODEL), lambda m: (0, 0))],
        out_specs=pl.BlockSpec((tm, D_MODEL), lambda m: (m, 0)),
        out_shape=jax.ShapeDtypeStruct((M_TOTAL, D_MODEL), F32),
        compiler_params=_params("arbitrary"),
        name="final_norm",
    )(x, g.reshape(1, D_MODEL))


def _ffn(x, norm_g, mods, layer, w_gate, w_up, w_down):
    act = _ffn_up(x, norm_g, mods, layer, w_gate, w_up)
    return _outproj([act], w_down, x, mods, layer, 5, 256, "ffn_down")


def kernel(x_prompt, x_sample, cache_attn_k, cache_attn_v, state_ret_fwd, state_ret_bwd, c, c_ctx, ada_w, ada_b, norm_mix_g, norm_ffn_g, even_w_in, even_w_out, attn_sink, sgu_ln_g, sgu_ln_b, sgu_w, sgu_b, ret_w_in, ret_w_out, ret_gn_g, ret_decay_fwd, ret_decay_bwd, ffn_w_gate, ffn_w_up, ffn_w_down, final_g):
    x = jnp.concatenate([x_prompt.reshape(M_PROMPT, D_MODEL), x_sample.reshape(M_SAMPLE, D_MODEL)], axis=0)
    cv = jnp.concatenate([c_ctx[None, :], c, jnp.zeros((COND_ROWS - N_COND, D_MODEL), F32)], axis=0)
    mods = _adaln(cv, ada_w, ada_b).reshape(DEPTH, COND_ROWS, 1, 6 * D_MODEL)

    z = _even_inproj(x, norm_mix_g[0], mods, 0, even_w_in[0])
    a_p = _attn_ctx(z, attn_sink[0])
    a_s = _attn_lat(z, attn_sink[0],
                    cache_attn_k[:, 0].reshape(DEC_BATCH, PAST_LEN, KV_WIDTH),
                    cache_attn_v[:, 0].reshape(DEC_BATCH, PAST_LEN, KV_WIDTH))
    a = jnp.concatenate([a_p, a_s], axis=0)
    gated = _sgu(z, sgu_ln_g[0], sgu_ln_b[0], sgu_w[0], sgu_b[0])
    x = _outproj([a, gated], even_w_out[0], x, mods, 0, 2, 512, "even_outproj")
    new_k = z[2, :M_PROMPT, :KV_WIDTH].reshape(BATCH, 1, SEQ, ATT_KV_HEADS, HEAD_DIM)
    new_v = z[2, :M_PROMPT, KV_WIDTH:].reshape(BATCH, 1, SEQ, ATT_KV_HEADS, HEAD_DIM)
    x = _ffn(x, norm_ffn_g[0], mods, 0, ffn_w_gate[0], ffn_w_up[0], ffn_w_down[0])

    zr = _ret_inproj(x, norm_mix_g[1], mods, 1, ret_w_in[0])
    og_p, new_sf, new_sb = _retention(zr, ret_gn_g[0], ret_decay_fwd[0], ret_decay_bwd[0],
                                      SEQ, 0, BATCH, emit_states=True)
    (og_s,) = _retention(zr, ret_gn_g[0], ret_decay_fwd[0], ret_decay_bwd[0],
                         DEC_SEQ, M_PROMPT // DEC_SEQ, DEC_BATCH,
                         s0=(state_ret_fwd, state_ret_bwd))
    og = jnp.concatenate([og_p, og_s], axis=0)
    x = _outproj([og], ret_w_out[0], x, mods, 1, 2, 256, "ret_outproj")
    x = _ffn(x, norm_ffn_g[1], mods, 1, ffn_w_gate[1], ffn_w_up[1], ffn_w_down[1])

    y = _final_norm(x, final_g)
    y_prompt = y[:M_PROMPT].reshape(BATCH, SEQ, D_MODEL)
    y_sample = y[M_PROMPT:].reshape(DEC_BATCH, DEC_SEQ, D_MODEL)
    return (y_prompt, y_sample, new_k, new_v, new_sf, new_sb)
```
